```python
import math
import jax, jax.numpy as jnp
from jax import lax
import numpy as np

D_MODEL = 2048
BATCH = 32
SEQ = 256
DEPTH = 2
DEC_BATCH = 4
DEC_SEQ = 2048
PAST_LEN = 512

GRID_W = 64
HEAD_DIM = 128
A_HEADS = D_MODEL // 4 // HEAD_DIM
A_DK = HEAD_DIM
A_DV = HEAD_DIM
MIX_A = A_HEADS * A_DV
A_COLS = 4 * MIX_A + 4 * A_HEADS
CONV_K = 5
CHUNK = 64
B_HEADS = D_MODEL // 4 // HEAD_DIM
B_QK = HEAD_DIM // 2
B_V = HEAD_DIM
MIX_B = B_HEADS * B_V
B_COLS = 3 * MIX_B
C_HEADS = D_MODEL // 2 // HEAD_DIM
C_KV = C_HEADS // 4
C_DIM = HEAD_DIM
MIX_C = C_HEADS * C_DIM
C_COLS = (C_HEADS + 2 * C_KV) * C_DIM
IN_COLS = A_COLS + B_COLS + C_COLS
MIX_WIDTH = MIX_A + MIX_B + MIX_C
Q_BLOCK = 128
ROPE_THETA = 10000.0
N_GROUPS = 4
EXPERTS_PER_GROUP = 4
N_EXPERTS = N_GROUPS * EXPERTS_PER_GROUP
EXPERT_FF = D_MODEL // 2
TOP_K = 2
DN_ALPHA = (2 * DEPTH) ** 0.25
DN_BETA = (8 * DEPTH) ** -0.25
EPS = 1e-6

kernel_name = 'hybrid_deltanet_diffattn_gqa_hmoe_step'


def layer_norm(x, g, b):
    xf = x.astype(jnp.float32)
    mu = jnp.mean(xf, -1, keepdims=True)
    var = jnp.mean(jnp.square(xf - mu), -1, keepdims=True)
    return ((xf - mu) * lax.rsqrt(var + EPS)).astype(x.dtype) * g + b


def rms_norm(x, w):
    xf = x.astype(jnp.float32)
    return (xf * lax.rsqrt(jnp.mean(xf * xf, -1, keepdims=True) + EPS)).astype(x.dtype) * w


def l2_norm(x):
    xf = x.astype(jnp.float32)
    return (xf * lax.rsqrt(jnp.sum(xf * xf, -1, keepdims=True) + EPS)).astype(x.dtype)


def to_heads(t, n):
    b, l, _ = t.shape
    return t.reshape(b, l, n, -1).transpose(0, 2, 1, 3)


def from_heads(t):
    b, n, l, d = t.shape
    return t.transpose(0, 2, 1, 3).reshape(b, l, n * d)


def grid_positions(length):
    rows = length // GRID_W
    row = jnp.repeat(jnp.arange(rows), GRID_W)
    col = jnp.tile(jnp.arange(GRID_W), rows)
    return row, col


def rope_1d(u, pos):
    m = u.shape[-1]
    freqs = ROPE_THETA ** (-jnp.arange(0, m, 2, dtype=jnp.float32) / m)
    ang = pos.astype(jnp.float32)[:, None] * freqs[None, :]
    cos = jnp.cos(ang).astype(u.dtype)
    sin = jnp.sin(ang).astype(u.dtype)
    u1, u2 = u[..., : m // 2], u[..., m // 2:]
    return jnp.concatenate([u1 * cos - u2 * sin, u1 * sin + u2 * cos], -1)


def axial_rope(x, row, col):
    half = x.shape[-1] // 2
    return jnp.concatenate([rope_1d(x[..., :half], row), rope_1d(x[..., half:], col)], -1)


def short_conv(u, w):
    out = lax.conv_general_dilated(
        u, w[:, None, :], window_strides=(1,), padding=[(CONV_K // 2, CONV_K // 2)],
        dimension_numbers=('NWC', 'WIO', 'NWC'), feature_group_count=u.shape[-1])
    return jax.nn.silu(out)


def block_attention(q, k, v):
    b, hkv, g, lq, d = q.shape
    nb = lq // Q_BLOCK
    qb = jnp.moveaxis(q.reshape(b, hkv, g, nb, Q_BLOCK, d), 3, 0)
    scale = d ** -0.5

    def one(q_blk):
        s = jnp.einsum('bhgqd,bhkd->bhgqk', q_blk, k, preferred_element_type=jnp.float32) * scale
        p = jax.nn.softmax(s, axis=-1)
        return jnp.einsum('bhgqk,bhkd->bhgqd', p.astype(v.dtype), v)

    o = lax.map(one, qb)
    return jnp.moveaxis(o, 0, 3).reshape(b, hkv, g, lq, -1)


def gated_delta_chunked(q, k, v, beta, g, s0):
    b, h, l, _ = q.shape
    dv = v.shape[-1]
    n = l // CHUNK
    f32 = jnp.float32
    q, k, v = (t.astype(f32).reshape(b, h, n, CHUNK, -1) for t in (q, k, v))
    beta, g = (t.astype(f32).reshape(b, h, n, CHUNK) for t in (beta, g))
    gc = jnp.cumsum(g, axis=-1)
    idx = jnp.arange(CHUNK)
    incl = idx[:, None] >= idx[None, :]
    strict = idx[:, None] > idx[None, :]
    decay = jnp.where(incl, jnp.exp(jnp.where(incl, gc[..., :, None] - gc[..., None, :], 0.0)), 0.0)
    m = jnp.where(strict, beta[..., :, None] * jnp.einsum('bhncd,bhnjd->bhncj', k, k) * decay, 0.0)
    eye = jnp.broadcast_to(jnp.eye(CHUNK, dtype=f32), m.shape)
    t_inv = lax.linalg.triangular_solve(eye + m, eye, left_side=True, lower=True, unit_diagonal=True)
    u = t_inv @ (v * beta[..., None])
    w = t_inv @ (k * (beta * jnp.exp(gc))[..., None])
    qk = jnp.einsum('bhncd,bhnjd->bhncj', q, k) * decay
    q_dec = q * jnp.exp(gc)[..., None]
    k_dec = k * jnp.exp(gc[..., -1:] - gc)[..., None]
    g_end = jnp.exp(gc[..., -1])
    xs = tuple(jnp.moveaxis(t, 2, 0) for t in (u, w, qk, q_dec, k_dec, g_end))

    def step(s, inp):
        u_c, w_c, qk_c, qd_c, kd_c, ge_c = inp
        delta = u_c - w_c @ s
        o = qd_c @ s + qk_c @ delta
        s = ge_c[..., None, None] * s + jnp.swapaxes(kd_c, -1, -2) @ delta
        return s, o

    s_fin, o = lax.scan(step, s0.astype(f32), xs)
    return jnp.moveaxis(o, 0, 2).reshape(b, h, l, dv), s_fin


def mixer_a(pa, lw, s0f, s0b):
    b, l, _ = pa.shape
    qkv = short_conv(pa[..., : 3 * MIX_A], lw['a_conv'])
    q, k, v = (to_heads(t, A_HEADS) for t in jnp.split(qkv, 3, axis=-1))
    q = l2_norm(q) * (A_DK ** -0.5)
    k = l2_norm(k)
    out_gate = pa[..., 3 * MIX_A: 4 * MIX_A]
    ba = jnp.transpose(pa[..., 4 * MIX_A:].reshape(b, l, 2, 2, A_HEADS), (2, 3, 0, 4, 1))
    beta = jax.nn.sigmoid(ba[0])
    g = -jnp.exp(lw['a_log'])[:, None, :, None] * jax.nn.softplus(ba[1] + lw['a_dt_bias'][:, None, :, None])
    flip = lambda t: jnp.flip(t, axis=2)
    o_f, s_f = gated_delta_chunked(q, k, v, beta[0], g[0], s0f)
    o_b, s_b = gated_delta_chunked(flip(q), flip(k), flip(v), flip(beta[1]), flip(g[1]), s0b)
    o = (o_f + flip(o_b)).astype(pa.dtype)
    o = from_heads(rms_norm(o, lw['a_norm'])) * jax.nn.silu(out_gate)
    return o, s_f, s_b


def mixer_b(pb, lw, lam_init, pos, ctx_kv):
    q, k, v = (to_heads(t, B_HEADS) for t in jnp.split(pb, 3, axis=-1))
    if pos is not None:
        row, col = pos
        q = jnp.concatenate([axial_rope(q[..., :B_QK], row, col), axial_rope(q[..., B_QK:], row, col)], -1)
        k = jnp.concatenate([axial_rope(k[..., :B_QK], row, col), axial_rope(k[..., B_QK:], row, col)], -1)
    if ctx_kv is None:
        k_all, v_all = k, v
    else:
        k_all = jnp.concatenate([ctx_kv[:, 0], k], axis=2)
        v_all = jnp.concatenate([ctx_kv[:, 1], v], axis=2)
    o1 = block_attention(q[:, :, None, :, :B_QK], k_all[..., :B_QK], v_all)[:, :, 0]
    o2 = block_attention(q[:, :, None, :, B_QK:], k_all[..., B_QK:], v_all)[:, :, 0]
    lam_p = lw['b_lambda'].astype(jnp.float32)
    lam = jnp.exp(jnp.sum(lam_p[0] * lam_p[1])) - jnp.exp(jnp.sum(lam_p[2] * lam_p[3])) + lam_init
    o = rms_norm(o1 - lam.astype(o1.dtype) * o2, lw['b_norm']) * (1.0 - lam_init)
    return from_heads(o), k, v


def mixer_c(pc, lw, pos, ctx_kv):
    b, l, _ = pc.shape
    q, k, v = jnp.split(pc, [MIX_C, MIX_C + C_KV * C_DIM], axis=-1)
    q = rms_norm(to_heads(q, C_HEADS), lw['c_q_norm'])
    k = rms_norm(to_heads(k, C_KV), lw['c_k_norm'])
    v = to_heads(v, C_KV)
    if pos is not None:
        q = axial_rope(q, pos[0], pos[1])
        k = axial_rope(k, pos[0], pos[1])
    if ctx_kv is None:
        k_all, v_all = k, v
    else:
        k_all = jnp.concatenate([ctx_kv[:, 0], k], axis=2)
        v_all = jnp.concatenate([ctx_kv[:, 1], v], axis=2)
    o = block_attention(q.reshape(b, C_KV, C_HEADS // C_KV, l, C_DIM), k_all, v_all)
    return from_heads(o.reshape(b, C_HEADS, l, C_DIM)), k, v


def token_mixing(h, lw, lam_init, ctx):
    b, l, _ = h.shape
    proj = h @ lw['w_in']
    pa, pb, pc = jnp.split(proj, [A_COLS, A_COLS + B_COLS], axis=-1)
    if ctx is None:
        s0 = jnp.zeros((b, A_HEADS, A_DK, A_DV), jnp.float32)
        s0f, s0b, pos, kv_b, kv_c = s0, s0, None, None, None
    else:
        st_a, kv_b, kv_c = ctx
        s0f, s0b = st_a[:, 0], st_a[:, 1]
        pos = grid_positions(l)
    o_a, s_f, s_b = mixer_a(pa, lw, s0f, s0b)
    o_b, k_b, v_b = mixer_b(pb, lw, lam_init, pos, kv_b)
    o_c, k_c, v_c = mixer_c(pc, lw, pos, kv_c)
    out = jnp.concatenate([o_a, o_b, o_c], axis=-1) @ lw['w_out']
    if ctx is None:
        ctx_new = (jnp.stack([s_f, s_b], 1).astype(h.dtype), jnp.stack([k_b, v_b], 1), jnp.stack([k_c, v_c], 1))
        return out, ctx_new
    return out, None


def hier_moe(h, lw):
    b, l, d = h.shape
    x = h.reshape(b * l, d)
    gl = (x @ lw['w_group'] + lw['b_group']).astype(jnp.float32)
    gv, gi = lax.top_k(jax.nn.softmax(gl, axis=-1), 1)
    el = (x @ lw['w_router'] + lw['b_router']).astype(jnp.float32).reshape(-1, N_GROUPS, EXPERTS_PER_GROUP)
    el = jnp.take_along_axis(el, gi[:, :, None], axis=1)[:, 0]
    ev, ei = lax.top_k(el, TOP_K)
    wts = jax.nn.softmax(ev, axis=-1) * gv
    eidx = gi * EXPERTS_PER_GROUP + ei
    gates = jnp.sum(jax.nn.one_hot(eidx, N_EXPERTS, dtype=jnp.float32) * wts[..., None], axis=1).astype(h.dtype)
    a = jnp.einsum('td,edf->tef', x, lw['w_gate'])
    u = jnp.einsum('td,edf->tef', x, lw['w_up'])
    y = jnp.einsum('tef,efd->td', jax.nn.silu(a) * u * gates[:, :, None], lw['w_down'])
    return y.reshape(b, l, d)


def trunk_layer(x, cond, lw, lam_init, ctx):
    mod = (jax.nn.silu(cond) @ lw['w_ada'] + lw['b_ada'])[:, None, :]
    sh1, sc1, g1, sh2, sc2, g2 = jnp.split(mod, 6, axis=-1)
    mix, ctx_new = token_mixing(x * (1.0 + sc1) + sh1, lw, lam_init, ctx)
    x = layer_norm(DN_ALPHA * x + g1 * mix, lw['ln_g'][0], lw['ln_b'][0])
    ffn = hier_moe(x * (1.0 + sc2) + sh2, lw)
    x = layer_norm(DN_ALPHA * x + g2 * ffn, lw['ln_g'][1], lw['ln_b'][1])
    return x, ctx_new


def setup_inputs(seed: int = 0) -> dict:
    key = jax.random.key(seed)
    ks = iter(jax.random.split(key, 32))
    f32 = jnp.float32
    nrm = lambda shape, s: jax.random.normal(next(ks), shape, f32) * s
    dsc = D_MODEL ** -0.5
    x_prompt = nrm((BATCH, SEQ, D_MODEL), 1.0)
    x_sample = nrm((DEC_BATCH, DEC_SEQ, D_MODEL), 1.0)
    state_a = nrm((DEC_BATCH, DEPTH, 2, A_HEADS, A_DK, A_DV), 0.1)
    cache_b_kv = nrm((DEC_BATCH, DEPTH, 2, B_HEADS, PAST_LEN, B_V), 1.0)
    cache_c_kv = nrm((DEC_BATCH, DEPTH, 2, C_KV, PAST_LEN, C_DIM), 1.0)
    c = nrm((DEC_BATCH, D_MODEL), 1.0)
    c_ctx = nrm((D_MODEL,), 1.0)
    w_ada = nrm((DEPTH, D_MODEL, 6 * D_MODEL), 0.5 * dsc)
    b_ada = nrm((DEPTH, 6 * D_MODEL), 0.02)
    w_in = nrm((DEPTH, D_MODEL, IN_COLS), dsc)
    a_conv = nrm((DEPTH, CONV_K, 3 * MIX_A), CONV_K ** -0.5)
    a_log = jnp.log(jax.random.uniform(next(ks), (DEPTH, 2, A_HEADS), f32, 1.0, 16.0))
    dt = jnp.exp(jax.random.uniform(next(ks), (DEPTH, 2, A_HEADS), f32, math.log(1e-3), math.log(1e-1)))
    a_dt_bias = dt + jnp.log(-jnp.expm1(-dt))
    a_norm = 1.0 + nrm((DEPTH, A_DV), 0.02)
    b_lambda = nrm((DEPTH, 4, B_QK), 0.1)
    b_norm = 1.0 + nrm((DEPTH, B_V), 0.02)
    c_q_norm = 1.0 + nrm((DEPTH, C_DIM), 0.02)
    c_k_norm = 1.0 + nrm((DEPTH, C_DIM), 0.02)
    w_out = nrm((DEPTH, MIX_WIDTH, D_MODEL), MIX_WIDTH ** -0.5 * DN_BETA)
    ln_g = 1.0 + nrm((DEPTH, 2, D_MODEL), 0.02)
    ln_b = nrm((DEPTH, 2, D_MODEL), 0.02)
    w_group = nrm((DEPTH, D_MODEL, N_GROUPS), dsc)
    b_group = nrm((DEPTH, N_GROUPS), 0.01)
    w_router = nrm((DEPTH, D_MODEL, N_EXPERTS), dsc)
    b_router = nrm((DEPTH, N_EXPERTS), 0.01)
    w_gate = nrm((DEPTH, N_EXPERTS, D_MODEL, EXPERT_FF), dsc)
    w_up = nrm((DEPTH, N_EXPERTS, D_MODEL, EXPERT_FF), dsc)
    w_down = nrm((DEPTH, N_EXPERTS, EXPERT_FF, D_MODEL), EXPERT_FF ** -0.5 * DN_BETA)
    return {'x_prompt': x_prompt, 'x_sample': x_sample, 'state_a': state_a, 'cache_b_kv': cache_b_kv,
            'cache_c_kv': cache_c_kv, 'c': c, 'c_ctx': c_ctx, 'w_ada': w_ada, 'b_ada': b_ada, 'w_in': w_in,
            'a_conv': a_conv, 'a_log': a_log, 'a_dt_bias': a_dt_bias, 'a_norm': a_norm, 'b_lambda': b_lambda,
            'b_norm': b_norm, 'c_q_norm': c_q_norm, 'c_k_norm': c_k_norm, 'w_out': w_out, 'ln_g': ln_g,
            'ln_b': ln_b, 'w_group': w_group, 'b_group': b_group, 'w_router': w_router, 'b_router': b_router,
            'w_gate': w_gate, 'w_up': w_up, 'w_down': w_down}


def reference(x_prompt, x_sample, state_a, cache_b_kv, cache_c_kv, c, c_ctx, w_ada, b_ada, w_in, a_conv,
              a_log, a_dt_bias, a_norm, b_lambda, b_norm, c_q_norm, c_k_norm, w_out, ln_g, ln_b, w_group,
              b_group, w_router, b_router, w_gate, w_up, w_down):
    y_p = x_prompt
    y_s = x_sample
    new_a, new_b, new_c = [], [], []
    for l in range(DEPTH):
        lw = dict(w_ada=w_ada[l], b_ada=b_ada[l], w_in=w_in[l], a_conv=a_conv[l], a_log=a_log[l],
                  a_dt_bias=a_dt_bias[l], a_norm=a_norm[l], b_lambda=b_lambda[l], b_norm=b_norm[l],
                  c_q_norm=c_q_norm[l], c_k_norm=c_k_norm[l], w_out=w_out[l], ln_g=ln_g[l], ln_b=ln_b[l],
                  w_group=w_group[l], b_group=b_group[l], w_router=w_router[l], b_router=b_router[l],
                  w_gate=w_gate[l], w_up=w_up[l], w_down=w_down[l])
        lam_init = 0.8 - 0.6 * math.exp(-0.3 * l)
        y_p, (s_a, kv_b, kv_c) = trunk_layer(y_p, c_ctx[None, :], lw, lam_init, None)
        new_a.append(s_a)
        new_b.append(kv_b)
        new_c.append(kv_c)
        y_s, _ = trunk_layer(y_s, c, lw, lam_init, (state_a[:, l], cache_b_kv[:, l], cache_c_kv[:, l]))
    return (y_p, y_s, jnp.stack(new_a, axis=1), jnp.stack(new_b, axis=1), jnp.stack(new_c, axis=1))
```

```python
import functools
import math

import jax
import jax.numpy as jnp
from jax import lax
from jax.experimental import pallas as pl
from jax.experimental.pallas import tpu as pltpu

F32 = jnp.float32
BF16 = jnp.bfloat16

D = 2048
BATCH, SEQ = 32, 256
DEC_BATCH, DEC_SEQ = 4, 2048
DEPTH = 2
PAST = 512
GRID_W = 64
HD = 128
A_HEADS = 4
B_HEADS = 4
B_QK = 64
C_HEADS = 8
C_KV = 2
C_GROUP = C_HEADS // C_KV
CONV_K = 5
CHUNK = 64
PAIR = 2 * CHUNK
N_GROUPS = 4
EPG = 4
N_EXPERTS = 16
EXPERT_FF = 1024
ROPE_THETA = 10000.0
DN_ALPHA = (2 * DEPTH) ** 0.25
EPS = 1e-6

T_P = BATCH * SEQ
T_S = DEC_BATCH * DEC_SEQ
T_ALL = T_P + T_S
N_COND = 8
A_MAIN = 4 * A_HEADS * HD
B_COLS = 3 * B_HEADS * HD
C_COLS = (C_HEADS + 2 * C_KV) * HD
N_MAIN = A_MAIN + B_COLS + C_COLS
N_BA = 4 * A_HEADS
N_PROJ = 5376
CB_AQ, CB_AK, CB_AV, CB_AG = 0, 4, 8, 12
CB_BQ, CB_BK, CB_BV = 16, 20, 24
CB_CQ, CB_CK, CB_CV = 28, 36, 38
CB_BA = 40

VMEM_LIMIT = 56 * 1024 * 1024
TM_IN, TN_IN = 512, 768
TM_OUT = 256
TM_MOE = 512
TM_FIN = 512
TN_ADA = 1024
N_TILES = (2 * T_ALL) // TM_MOE + N_EXPERTS
P_ROWS = N_TILES * TM_MOE


def _cparams(sem):
    return pltpu.CompilerParams(dimension_semantics=sem, vmem_limit_bytes=VMEM_LIMIT)


def _sigmoid(x):
    return 1.0 / (1.0 + jnp.exp(-x))


def _silu(x):
    return x * _sigmoid(x)


def _softplus(x):
    return jnp.maximum(x, 0.0) + jnp.log1p(jnp.exp(-jnp.abs(x)))


def _bdot(a, b):
    return jnp.dot(a.astype(BF16), b.astype(BF16), preferred_element_type=F32)


def _bdot_nt(a, b):
    return lax.dot_general(a.astype(BF16), b.astype(BF16), (((1,), (1,)), ((), ())),
                           preferred_element_type=F32)


def _layer_norm(x, g, b):
    mu = jnp.mean(x, axis=-1, keepdims=True)
    xc = x - mu
    var = jnp.mean(xc * xc, axis=-1, keepdims=True)
    return xc * lax.rsqrt(var + EPS) * g + b


def _rms(x, w):
    return x * lax.rsqrt(jnp.mean(x * x, axis=-1, keepdims=True) + EPS) * w


def _cond_row(i, tm):
    n_p = T_P // tm
    per_b = DEC_SEQ // tm
    return jnp.where(i < n_p, 0, 1 + (i - n_p) // per_b)


def _ada_kernel(c_ref, w_ref, b_ref, o_ref):
    s = _silu(c_ref[...])
    o_ref[0] = _bdot(s, w_ref[0]) + b_ref[0]


def _ada_call(cond, w_ada, b_ada):
    n = 6 * D
    return pl.pallas_call(
        _ada_kernel,
        grid=(DEPTH, n // TN_ADA),
        in_specs=[
            pl.BlockSpec((N_COND, D), lambda l, j: (0, 0)),
            pl.BlockSpec((1, D, TN_ADA), lambda l, j: (l, 0, j)),
            pl.BlockSpec((1, 1, TN_ADA), lambda l, j: (l, 0, j)),
        ],
        out_specs=pl.BlockSpec((1, N_COND, TN_ADA), lambda l, j: (l, 0, j)),
        out_shape=jax.ShapeDtypeStruct((DEPTH, N_COND, n), F32),
        compiler_params=_cparams(("parallel", "parallel")),
        name="ada_mod",
    )(cond, w_ada, b_ada.reshape(DEPTH, 1, n))


def _inproj_kernel(x_ref, sh_ref, sc_ref, w_ref, o_ref, h_scr):
    @pl.when(pl.program_id(1) == 0)
    def _():
        h_scr[...] = (x_ref[...] * (1.0 + sc_ref[0]) + sh_ref[0]).astype(BF16)

    o_ref[...] = jnp.dot(h_scr[...], w_ref[...], preferred_element_type=F32)


def _inproj_call(x, mod, w):
    return pl.pallas_call(
        _inproj_kernel,
        grid=(T_ALL // TM_IN, N_PROJ // TN_IN),
        in_specs=[
            pl.BlockSpec((TM_IN, D), lambda i, j: (i, 0)),
            pl.BlockSpec((1, 1, D), lambda i, j: (0 * N_COND + _cond_row(i, TM_IN), 0, 0)),
            pl.BlockSpec((1, 1, D), lambda i, j: (1 * N_COND + _cond_row(i, TM_IN), 0, 0)),
            pl.BlockSpec((D, TN_IN), lambda i, j: (0, j)),
        ],
        out_specs=pl.BlockSpec((TM_IN, TN_IN), lambda i, j: (i, j)),
        out_shape=jax.ShapeDtypeStruct((T_ALL, N_PROJ), F32),
        scratch_shapes=[pltpu.VMEM((TM_IN, D), BF16)],
        compiler_params=_cparams(("parallel", "arbitrary")),
        name="in_proj",
    )(x, mod, mod, w)


def _rope_tables(length, m, n_rep):
    rows = length // GRID_W
    row = jnp.repeat(jnp.arange(rows), GRID_W)
    col = jnp.tile(jnp.arange(GRID_W), rows)
    freqs = ROPE_THETA ** (-jnp.arange(0, m, 2, dtype=F32) / m)

    def one(pos):
        ang = pos.astype(F32)[:, None] * freqs[None, :]
        c, s = jnp.cos(ang), jnp.sin(ang)
        return jnp.concatenate([c, c], -1), jnp.concatenate([-s, s], -1)

    cr, sr = one(row)
    cc, sc = one(col)
    cos = jnp.concatenate([cr, cc] * n_rep, -1)
    sin = jnp.concatenate([sr, sc] * n_rep, -1)
    return cos, sin


def _rope(x, cos, sin, half):
    lane = lax.broadcasted_iota(jnp.int32, x.shape, 1)
    lower = (lane & (2 * half - 1)) < half
    partner = jnp.where(lower, pltpu.roll(x, HD - half, axis=1), pltpu.roll(x, half, axis=1))
    return x * cos + partner * sin


def _softmax_pv(qst, kt_scr, v_scr):
    s = jnp.dot(qst, kt_scr[...], preferred_element_type=F32)
    m = jnp.max(s, axis=-1, keepdims=True)
    p = jnp.exp(s - m)
    l = jnp.sum(p, axis=-1, keepdims=True)
    o = jnp.dot(p.astype(BF16), v_scr[...], preferred_element_type=F32)
    return o / l


def _attn_c_kernel(*refs, L, n_ctx, tq, rope, write_cache):
    refs = list(refs)
    q_ref, k_ref, v_ref = refs[:3]
    refs = refs[3:]
    if n_ctx:
        kc_ref, vc_ref = refs[:2]
        refs = refs[2:]
    qn_ref, kn_ref = refs[:2]
    refs = refs[2:]
    if rope:
        cosq_ref, sinq_ref, cosk_ref, sink_ref = refs[:4]
        refs = refs[4:]
    o_ref = refs[0]
    refs = refs[1:]
    if write_cache:
        cache_ref = refs[0]
        refs = refs[1:]
    kt_scr, v_scr = refs

    @pl.when(pl.program_id(2) == 0)
    def _():
        k = _rms(k_ref[...], kn_ref[...])
        v = v_ref[...]
        if write_cache:
            cache_ref[0] = k
            cache_ref[1] = v
        if rope:
            k = _rope(k, cosk_ref[...], sink_ref[...], 32)
        if n_ctx:
            kt_scr[:, :n_ctx] = kc_ref[...].T.astype(BF16)
            v_scr[:n_ctx, :] = vc_ref[...].astype(BF16)
        kt_scr[:, n_ctx:] = k.T.astype(BF16)
        v_scr[n_ctx:, :] = v.astype(BF16)

    scale = HD ** -0.5
    qs = []
    for j in range(C_GROUP):
        qj = _rms(q_ref[:, j * HD:(j + 1) * HD], qn_ref[...])
        if rope:
            qj = _rope(qj, cosq_ref[...], sinq_ref[...], 32)
        qs.append((qj * scale).astype(BF16))
    o = _softmax_pv(jnp.concatenate(qs, axis=0), kt_scr, v_scr)
    for j in range(C_GROUP):
        o_ref[:, j * HD:(j + 1) * HD] = o[j * tq:(j + 1) * tq].astype(BF16)


def _attn_c_call(proj, layer, qn, kn, *, stream, cache=None, tables=None):
    if stream == "p":
        nb, L, n_ctx, tq, rb0 = BATCH, SEQ, 0, SEQ, 0
    else:
        nb, L, n_ctx, tq, rb0 = DEC_BATCH, DEC_SEQ, PAST, 128, T_P // DEC_SEQ
    nq = L // tq
    rope = tables is not None
    write_cache = stream == "p"
    qb0 = rb0 * nq
    in_specs = [
        pl.BlockSpec((tq, C_GROUP * HD), lambda b, g, i: (qb0 + b * nq + i, CB_CQ // C_GROUP + g)),
        pl.BlockSpec((L, HD), lambda b, g, i: (rb0 + b, CB_CK + g)),
        pl.BlockSpec((L, HD), lambda b, g, i: (rb0 + b, CB_CV + g)),
    ]
    args = [proj, proj, proj]
    if n_ctx:
        in_specs += [
            pl.BlockSpec((None, None, None, None, PAST, HD), lambda b, g, i: (b, layer, 0, g, 0, 0)),
            pl.BlockSpec((None, None, None, None, PAST, HD), lambda b, g, i: (b, layer, 1, g, 0, 0)),
        ]
        args += [cache, cache]
    in_specs += [pl.BlockSpec((1, HD), lambda b, g, i: (0, 0))] * 2
    args += [qn, kn]
    if rope:
        cos, sin = tables
        in_specs += [
            pl.BlockSpec((tq, HD), lambda b, g, i: (i, 0)),
            pl.BlockSpec((tq, HD), lambda b, g, i: (i, 0)),
            pl.BlockSpec((L, HD), lambda b, g, i: (0, 0)),
            pl.BlockSpec((L, HD), lambda b, g, i: (0, 0)),
        ]
        args += [cos, sin, cos, sin]
    out_specs = [pl.BlockSpec((tq, C_GROUP * HD), lambda b, g, i: (b * nq + i, g))]
    out_shape = [jax.ShapeDtypeStruct((nb * L, C_HEADS * HD), BF16)]
    if write_cache:
        out_specs.append(pl.BlockSpec((None, 2, None, L, HD), lambda b, g, i: (b, 0, g, 0, 0)))
        out_shape.append(jax.ShapeDtypeStruct((nb, 2, C_KV, L, HD), F32))
    return pl.pallas_call(
        functools.partial(_attn_c_kernel, L=L, n_ctx=n_ctx, tq=tq, rope=rope, write_cache=write_cache),
        grid=(nb, C_KV, nq),
        in_specs=in_specs,
        out_specs=out_specs,
        out_shape=out_shape,
        scratch_shapes=[pltpu.VMEM((HD, n_ctx + L), BF16), pltpu.VMEM((n_ctx + L, HD), BF16)],
        compiler_params=_cparams(("parallel", "parallel", "arbitrary")),
        name="attn_c_" + stream,
    )(*args)


def _attn_b_kernel(*refs, L, n_ctx, tq, rope, write_cache, lam_init):
    refs = list(refs)
    q_ref, k_ref, v_ref = refs[:3]
    refs = refs[3:]
    if n_ctx:
        kc_ref, vc_ref = refs[:2]
        refs = refs[2:]
    lam_ref, bn_ref = refs[:2]
    refs = refs[2:]
    if rope:
        cosq_ref, sinq_ref, cosk_ref, sink_ref = refs[:4]
        refs = refs[4:]
    o_ref = refs[0]
    refs = refs[1:]
    if write_cache:
        cache_ref = refs[0]
        refs = refs[1:]
    kt_scr, v_scr = refs

    @pl.when(pl.program_id(2) == 0)
    def _():
        k = k_ref[...]
        v = v_ref[...]
        if write_cache:
            cache_ref[0] = k
            cache_ref[1] = v
        if rope:
            k = _rope(k, cosk_ref[...], sink_ref[...], 16)
        if n_ctx:
            kt_scr[:, :n_ctx] = kc_ref[...].T.astype(BF16)
            v_scr[:n_ctx, :] = vc_ref[...].astype(BF16)
        kt_scr[:, n_ctx:] = k.T.astype(BF16)
        v_scr[n_ctx:, :] = v.astype(BF16)

    q = q_ref[...]
    if rope:
        q = _rope(q, cosq_ref[...], sinq_ref[...], 16)
    q = q * (B_QK ** -0.5)
    lane = lax.broadcasted_iota(jnp.int32, q.shape, 1)
    q1 = jnp.where(lane < B_QK, q, 0.0).astype(BF16)
    q2 = jnp.where(lane >= B_QK, q, 0.0).astype(BF16)
    o = _softmax_pv(jnp.concatenate([q1, q2], axis=0), kt_scr, v_scr)
    lp = lam_ref[...]
    lam = (jnp.exp(jnp.sum(lp[0:1] * lp[1:2], axis=-1, keepdims=True))
           - jnp.exp(jnp.sum(lp[2:3] * lp[3:4], axis=-1, keepdims=True)) + lam_init)
    d = o[:tq] - lam * o[tq:]
    o_ref[...] = (_rms(d, bn_ref[...]) * (1.0 - lam_init)).astype(BF16)


def _attn_b_call(proj, layer, lam_p, bn, lam_init, *, stream, cache=None, tables=None):
    if stream == "p":
        nb, L, n_ctx, tq, rb0 = BATCH, SEQ, 0, SEQ, 0
    else:
        nb, L, n_ctx, tq, rb0 = DEC_BATCH, DEC_SEQ, PAST, 256, T_P // DEC_SEQ
    nq = L // tq
    rope = tables is not None
    write_cache = stream == "p"
    qb0 = rb0 * nq
    in_specs = [
        pl.BlockSpec((tq, HD), lambda b, h, i: (qb0 + b * nq + i, CB_BQ + h)),
        pl.BlockSpec((L, HD), lambda b, h, i: (rb0 + b, CB_BK + h)),
        pl.BlockSpec((L, HD), lambda b, h, i: (rb0 + b, CB_BV + h)),
    ]
    args = [proj, proj, proj]
    if n_ctx:
        in_specs += [
            pl.BlockSpec((None, None, None, None, PAST, HD), lambda b, h, i: (b, layer, 0, h, 0, 0)),
            pl.BlockSpec((None, None, None, None, PAST, HD), lambda b, h, i: (b, layer, 1, h, 0, 0)),
        ]
        args += [cache, cache]
    in_specs += [pl.BlockSpec((4, B_QK), lambda b, h, i: (0, 0)), pl.BlockSpec((1, HD), lambda b, h, i: (0, 0))]
    args += [lam_p, bn]
    if rope:
        cos, sin = tables
        in_specs += [
            pl.BlockSpec((tq, HD), lambda b, h, i: (i, 0)),
            pl.BlockSpec((tq, HD), lambda b, h, i: (i, 0)),
            pl.BlockSpec((L, HD), lambda b, h, i: (0, 0)),
            pl.BlockSpec((L, HD), lambda b, h, i: (0, 0)),
        ]
        args += [cos, sin, cos, sin]
    out_specs = [pl.BlockSpec((tq, HD), lambda b, h, i: (b * nq + i, h))]
    out_shape = [jax.ShapeDtypeStruct((nb * L, B_HEADS * HD), BF16)]
    if write_cache:
        out_specs.append(pl.BlockSpec((None, 2, None, L, HD), lambda b, h, i: (b, 0, h, 0, 0)))
        out_shape.append(jax.ShapeDtypeStruct((nb, 2, B_HEADS, L, HD), F32))
    return pl.pallas_call(
        functools.partial(_attn_b_kernel, L=L, n_ctx=n_ctx, tq=tq, rope=rope, write_cache=write_cache,
                          lam_init=lam_init),
        grid=(nb, B_HEADS, nq),
        in_specs=in_specs,
        out_specs=out_specs,
        out_shape=out_shape,
        scratch_shapes=[pltpu.VMEM((HD, n_ctx + L), BF16), pltpu.VMEM((n_ctx + L, HD), BF16)],
        compiler_params=_cparams(("parallel", "parallel", "arbitrary")),
        name="attn_b_" + stream,
    )(*args)


def _tri_inverse(m, ii, jj):
    eye = jnp.where(ii == jj, 1.0, 0.0)
    m8 = jnp.where((ii >> 3) == (jj >> 3), m, 0.0)
    p2 = _bdot(m8, m8)
    p4 = _bdot(p2, p2)
    x = _bdot(_bdot(eye - m8, eye + p2), eye + p4)
    for sh in (3, 4, 5):
        off = ((ii >> (sh + 1)) == (jj >> (sh + 1))) & ((ii >> sh) != (jj >> sh))
        x = x - _bdot(_bdot(x, jnp.where(off, m, 0.0)), x)
    return x


def _delta_kernel(*refs, L, has_s0):
    refs = list(refs)
    q_ref, k_ref, v_ref, gate_ref, ba_ref, wq_ref, wk_ref, wv_ref, alog_ref, dtb_ref, an_ref = refs[:11]
    refs = refs[11:]
    if has_s0:
        s0_ref = refs[0]
        refs = refs[1:]
    o_ref = refs[0]
    refs = refs[1:]
    if not has_s0:
        sout_ref = refs[0]
        refs = refs[1:]
    q_scr, k_scr, v_scr, cg_scr, bt_scr, o_scr = refs

    h = pl.program_id(1)
    n_pairs = L // PAIR
    row = lax.broadcasted_iota(jnp.int32, (L, HD), 0)
    lane = lax.broadcasted_iota(jnp.int32, (L, HD), 1)

    def conv_silu(x, w):
        acc = x * w[2:3]
        for j in (0, 1, 3, 4):
            sh = CONV_K // 2 - j
            src = row - sh
            shifted = jnp.where((src >= 0) & (src < L), pltpu.roll(x, sh % L, axis=0), 0.0)
            acc = acc + shifted * w[j:j + 1]
        return _silu(acc)

    q = conv_silu(q_ref[...], wq_ref[...])
    k = conv_silu(k_ref[...], wk_ref[...])
    v_scr[...] = conv_silu(v_ref[...], wv_ref[...])
    q_scr[...] = q * lax.rsqrt(jnp.sum(q * q, axis=-1, keepdims=True) + EPS) * (HD ** -0.5)
    k_scr[...] = k * lax.rsqrt(jnp.sum(k * k, axis=-1, keepdims=True) + EPS)

    ba = ba_ref[...]
    beta_all = _sigmoid(ba)
    g_all = -jnp.exp(alog_ref[...]) * _softplus(ba + dtb_ref[...])
    pos = row & (CHUNK - 1)
    for d in range(2):
        col_b = d * A_HEADS + h
        col_g = 2 * A_HEADS + d * A_HEADS + h
        beta = jnp.sum(jnp.where(lane == col_b, beta_all, 0.0), axis=-1, keepdims=True)
        g = jnp.sum(jnp.where(lane == col_g, g_all, 0.0), axis=-1, keepdims=True)
        bt_scr[d] = jnp.broadcast_to(beta, (L, HD))
        gc = jnp.broadcast_to(g, (L, HD))
        for s in (1, 2, 4, 8, 16, 32):
            if d == 0:
                gc = gc + jnp.where(pos >= s, pltpu.roll(gc, s, axis=0), 0.0)
            else:
                gc = gc + jnp.where(pos < CHUNK - s, pltpu.roll(gc, L - s, axis=0), 0.0)
        cg_scr[d] = gc

    ii = lax.broadcasted_iota(jnp.int32, (PAIR, PAIR), 0)
    jj = lax.broadcasted_iota(jnp.int32, (PAIR, PAIR), 1)
    same = (ii >> 6) == (jj >> 6)
    first = ii < CHUNK
    zeros_c = jnp.zeros((CHUNK, HD), F32)

    def pair_step(p_idx, d, s):
        rows = pl.ds(pl.multiple_of(p_idx * PAIR, PAIR), PAIR)
        kp, qp, vp = k_scr[rows, :], q_scr[rows, :], v_scr[rows, :]
        c = cg_scr[d, rows, :]
        bt = bt_scr[d, rows, :]
        r = c.T
        incl = same & ((ii >= jj) if d == 0 else (ii <= jj))
        strict = same & ((ii > jj) if d == 0 else (ii < jj))
        decay = jnp.where(incl, jnp.exp(jnp.where(incl, c - r, 0.0)), 0.0)
        m = jnp.where(strict, bt * _bdot_nt(kp, kp) * decay, 0.0)
        t_inv = _tri_inverse(m, ii, jj)
        egc = jnp.exp(c)
        uw = _bdot(t_inv, jnp.concatenate([vp * bt, kp * (bt * egc)], axis=1))
        u, w = uw[:, :HD], uw[:, HD:]
        qk = _bdot_nt(qp, kp) * decay
        qd = qp * egc
        l0, l1 = (CHUNK - 1, PAIR - 1) if d == 0 else (0, CHUNK)
        g_last = jnp.where(first, r[:, l0:l0 + 1], r[:, l1:l1 + 1])
        kdt = (kp * jnp.exp(g_last - c)).T
        g_end = jnp.exp(g_last)
        outs = [None, None]
        for cidx in ((0, 1) if d == 0 else (1, 0)):
            sl = slice(cidx * CHUNK, (cidx + 1) * CHUNK)
            delta = u[sl] - _bdot(w[sl], s)
            dpad = jnp.concatenate([delta, zeros_c] if cidx == 0 else [zeros_c, delta], axis=0)
            outs[cidx] = _bdot(qd[sl], s) + _bdot(qk[sl], dpad)
            s = g_end[cidx * CHUNK:cidx * CHUNK + 1, :] * s + _bdot(kdt, dpad)
        return jnp.concatenate(outs, axis=0), s, rows

    def body(p, carry):
        s_f, s_b = carry
        o_f, s_f, rows_f = pair_step(p, 0, s_f)
        o_scr[0, rows_f, :] = o_f
        o_b, s_b, rows_b = pair_step(n_pairs - 1 - p, 1, s_b)
        o_scr[1, rows_b, :] = o_b
        return s_f, s_b

    if has_s0:
        init = (s0_ref[0], s0_ref[1])
    else:
        init = (jnp.zeros((HD, HD), F32), jnp.zeros((HD, HD), F32))
    s_f, s_b = lax.fori_loop(0, n_pairs, body, init)
    if not has_s0:
        sout_ref[0] = s_f
        sout_ref[1] = s_b
    o = _rms(o_scr[0] + o_scr[1], an_ref[...])
    o_ref[...] = (o * _silu(gate_ref[...])).astype(BF16)


def _delta_call(proj, layer, conv_w, alog_row, dtb_row, an, *, stream, state=None):
    if stream == "p":
        nb, L, rb0 = BATCH, SEQ, 0
    else:
        nb, L, rb0 = DEC_BATCH, DEC_SEQ, T_P // DEC_SEQ
    has_s0 = state is not None
    col = lambda cb: pl.BlockSpec((L, HD), lambda b, h: (rb0 + b, cb + h))
    in_specs = [col(CB_AQ), col(CB_AK), col(CB_AV), col(CB_AG),
                pl.BlockSpec((L, HD), lambda b, h: (rb0 + b, CB_BA)),
                pl.BlockSpec((8, HD), lambda b, h: (0, h)),
                pl.BlockSpec((8, HD), lambda b, h: (0, A_HEADS + h)),
                pl.BlockSpec((8, HD), lambda b, h: (0, 2 * A_HEADS + h)),
                pl.BlockSpec((1, HD), lambda b, h: (0, 0)),
                pl.BlockSpec((1, HD), lambda b, h: (0, 0)),
                pl.BlockSpec((1, HD), lambda b, h: (0, 0))]
    args = [proj] * 5 + [conv_w] * 3 + [alog_row, dtb_row, an]
    if has_s0:
        in_specs.append(pl.BlockSpec((None, None, 2, None, HD, HD), lambda b, h: (b, layer, 0, h, 0, 0)))
        args.append(state)
    out_specs = [pl.BlockSpec((L, HD), lambda b, h: (b, h))]
    out_shape = [jax.ShapeDtypeStruct((nb * L, A_HEADS * HD), BF16)]
    if not has_s0:
        out_specs.append(pl.BlockSpec((None, 2, None, HD, HD), lambda b, h: (b, 0, h, 0, 0)))
        out_shape.append(jax.ShapeDtypeStruct((nb, 2, A_HEADS, HD, HD), F32))
    return pl.pallas_call(
        functools.partial(_delta_kernel, L=L, has_s0=has_s0),
        grid=(nb, A_HEADS),
        in_specs=in_specs,
        out_specs=out_specs,
        out_shape=out_shape,
        scratch_shapes=[pltpu.VMEM((L, HD), F32), pltpu.VMEM((L, HD), F32), pltpu.VMEM((L, HD), F32),
                        pltpu.VMEM((2, L, HD), F32), pltpu.VMEM((2, L, HD), F32), pltpu.VMEM((2, L, HD), F32)],
        compiler_params=_cparams(("parallel", "parallel")),
        name="delta_" + stream,
    )(*args)


def _outproj_kernel(oa_ref, ob_ref, oc_ref, x_ref, wa_ref, wb_ref, wc_ref, g1_ref, sh2_ref, sc2_ref,
                    lng_ref, lnb_ref, wr_ref, br_ref, x1_ref, h2_ref, rt_ref):
    mix = (jnp.dot(oa_ref[...], wa_ref[...], preferred_element_type=F32)
           + jnp.dot(ob_ref[...], wb_ref[...], preferred_element_type=F32)
           + jnp.dot(oc_ref[...], wc_ref[...], preferred_element_type=F32))
    x1 = _layer_norm(DN_ALPHA * x_ref[...] + g1_ref[0] * mix, lng_ref[...], lnb_ref[...])
    x1_ref[...] = x1
    h2 = x1 * (1.0 + sc2_ref[0]) + sh2_ref[0]
    h2_ref[...] = h2.astype(BF16)

    logits = jnp.dot(h2, wr_ref[...], preferred_element_type=F32, precision=lax.Precision.HIGHEST) + br_ref[...]
    lane = lax.broadcasted_iota(jnp.int32, logits.shape, 1)
    big = jnp.int32(1 << 20)
    neg = jnp.float32(-jnp.inf)
    is_g = lane < N_GROUPS
    gl = jnp.where(is_g, logits, neg)
    gmax = jnp.max(gl, axis=-1, keepdims=True)
    gi = jnp.min(jnp.where(gl == gmax, lane, big), axis=-1, keepdims=True)
    gv = 1.0 / jnp.sum(jnp.where(is_g, jnp.exp(gl - gmax), 0.0), axis=-1, keepdims=True)
    e_lo = N_GROUPS + gi * EPG
    in_grp = (lane >= e_lo) & (lane < e_lo + EPG)
    el = jnp.where(in_grp, logits, neg)
    v0 = jnp.max(el, axis=-1, keepdims=True)
    i0 = jnp.min(jnp.where(el == v0, lane, big), axis=-1, keepdims=True)
    el1 = jnp.where(lane == i0, neg, el)
    v1 = jnp.max(el1, axis=-1, keepdims=True)
    i1 = jnp.min(jnp.where(el1 == v1, lane, big), axis=-1, keepdims=True)
    e1 = jnp.exp(v1 - v0)
    w0 = gv / (1.0 + e1)
    w1 = gv * e1 / (1.0 + e1)
    rt = jnp.where(lane == 0, (i0 - N_GROUPS).astype(F32),
                   jnp.where(lane == 1, (i1 - N_GROUPS).astype(F32),
                             jnp.where(lane == 2, w0, jnp.where(lane == 3, w1, 0.0))))
    rt_ref[...] = rt


def _outproj_call(o_a, o_b, o_c, x, w_out, mod, ln_g, ln_b, w_rt, b_rt):
    tm = TM_OUT
    modspec = lambda which: pl.BlockSpec((1, 1, D), lambda i: (which * N_COND + _cond_row(i, tm), 0, 0))
    full = lambda shape: pl.BlockSpec(shape, lambda i: (0,) * len(shape))
    return pl.pallas_call(
        _outproj_kernel,
        grid=(T_ALL // tm,),
        in_specs=[
            pl.BlockSpec((tm, A_HEADS * HD), lambda i: (i, 0)),
            pl.BlockSpec((tm, B_HEADS * HD), lambda i: (i, 0)),
            pl.BlockSpec((tm, C_HEADS * HD), lambda i: (i, 0)),
            pl.BlockSpec((tm, D), lambda i: (i, 0)),
            pl.BlockSpec((A_HEADS * HD, D), lambda i: (0, 0)),
            pl.BlockSpec((B_HEADS * HD, D), lambda i: (1, 0)),
            pl.BlockSpec((C_HEADS * HD, D), lambda i: (1, 0)),
            modspec(2), modspec(3), modspec(4),
            full((1, D)), full((1, D)), full((D, HD)), full((1, HD)),
        ],
        out_specs=[
            pl.BlockSpec((tm, D), lambda i: (i, 0)),
            pl.BlockSpec((tm, D), lambda i: (i, 0)),
            pl.BlockSpec((tm, HD), lambda i: (i, 0)),
        ],
        out_shape=[jax.ShapeDtypeStruct((T_ALL, D), F32), jax.ShapeDtypeStruct((T_ALL, D), BF16),
                   jax.ShapeDtypeStruct((T_ALL, HD), F32)],
        compiler_params=_cparams(("parallel",)),
        name="out_proj",
    )(o_a, o_b, o_c, x, w_out, w_out, w_out, mod, mod, mod, ln_g, ln_b, w_rt, b_rt)


def _moe_kernel(te_ref, nv_ref, x_ref, wg_ref, wu_ref, wd_ref, o_ref):
    @pl.when(pl.program_id(0) < nv_ref[0])
    def _():
        x = x_ref[...]
        a = jnp.dot(x, wg_ref[...], preferred_element_type=F32)
        u = jnp.dot(x, wu_ref[...], preferred_element_type=F32)
        o_ref[...] = jnp.dot((_silu(a) * u).astype(BF16), wd_ref[...], preferred_element_type=F32)


def _moe_call(tile_expert, n_valid, xs, wg, wu, wd):
    tile = lambda i, te, nv: (jnp.minimum(i, nv[0] - 1), 0)
    wmap = lambda i, te, nv: (te[jnp.minimum(i, nv[0] - 1)], 0, 0)
    grid_spec = pltpu.PrefetchScalarGridSpec(
        num_scalar_prefetch=2,
        grid=(N_TILES,),
        in_specs=[
            pl.BlockSpec((TM_MOE, D), tile),
            pl.BlockSpec((None, D, EXPERT_FF), wmap),
            pl.BlockSpec((None, D, EXPERT_FF), wmap),
            pl.BlockSpec((None, EXPERT_FF, D), wmap),
        ],
        out_specs=pl.BlockSpec((TM_MOE, D), tile),
    )
    return pl.pallas_call(
        _moe_kernel,
        grid_spec=grid_spec,
        out_shape=jax.ShapeDtypeStruct((P_ROWS, D), F32),
        compiler_params=_cparams(("arbitrary",)),
        name="experts",
    )(tile_expert, n_valid, xs, wg, wu, wd)


def _dispatch_plan(route):
    e = route[:, :2].astype(jnp.int32).reshape(-1)
    onehot = (e[:, None] == jnp.arange(N_EXPERTS, dtype=jnp.int32)[None, :]).astype(jnp.int32)
    csum = jnp.cumsum(onehot, axis=0)
    rank = jnp.sum(onehot * csum, axis=1) - 1
    counts = csum[-1]
    tiles_per = (counts + TM_MOE - 1) // TM_MOE
    tile_end = jnp.cumsum(tiles_per)
    tile_start = tile_end - tiles_per
    dest = tile_start[e] * TM_MOE + rank
    n_valid = tile_end[-1:].astype(jnp.int32)
    tile_expert = jnp.minimum(
        jnp.searchsorted(tile_end, jnp.arange(N_TILES, dtype=jnp.int32), side="right"), N_EXPERTS - 1
    ).astype(jnp.int32)
    src = jnp.zeros((P_ROWS,), jnp.int32).at[dest].set(jnp.arange(2 * T_ALL, dtype=jnp.int32) // 2)
    return dest, src, tile_expert, n_valid


def _final_kernel(x1_ref, y0_ref, y1_ref, rt_ref, g2_ref, lng_ref, lnb_ref, o_ref):
    rt = rt_ref[...]
    ffn = rt[:, 2:3] * y0_ref[...] + rt[:, 3:4] * y1_ref[...]
    o_ref[...] = _layer_norm(DN_ALPHA * x1_ref[...] + g2_ref[0] * ffn, lng_ref[...], lnb_ref[...])


def _final_call(x1, y0, y1, route, mod, ln_g, ln_b):
    tm = TM_FIN
    tok = pl.BlockSpec((tm, D), lambda i: (i, 0))
    return pl.pallas_call(
        _final_kernel,
        grid=(T_ALL // tm,),
        in_specs=[tok, tok, tok, pl.BlockSpec((tm, HD), lambda i: (i, 0)),
                  pl.BlockSpec((1, 1, D), lambda i: (5 * N_COND + _cond_row(i, tm), 0, 0)),
                  pl.BlockSpec((1, D), lambda i: (0, 0)), pl.BlockSpec((1, D), lambda i: (0, 0))],
        out_specs=tok,
        out_shape=jax.ShapeDtypeStruct((T_ALL, D), F32),
        compiler_params=_cparams(("parallel",)),
        name="final_norm",
    )(x1, y0, y1, route, mod, ln_g, ln_b)


def _prep_w_in(w):
    a, b = A_MAIN, A_MAIN + N_BA
    w = jnp.concatenate([w[:, :a], w[:, b:], w[:, a:b]], axis=1)
    return jnp.pad(w, ((0, 0), (0, N_PROJ - w.shape[1]))).astype(BF16)


def _ba_row(p):
    return jnp.zeros((1, HD), F32).at[0, 2 * A_HEADS:4 * A_HEADS].set(p.reshape(-1))


def kernel(x_prompt, x_sample, state_a, cache_b_kv, cache_c_kv, c, c_ctx, w_ada, b_ada, w_in, a_conv, a_log, a_dt_bias, a_norm, b_lambda, b_norm, c_q_norm, c_k_norm, w_out, ln_g, ln_b, w_group, b_group, w_router, b_router, w_gate, w_up, w_down):
    x = jnp.concatenate([x_prompt.reshape(T_P, D), x_sample.reshape(T_S, D)], axis=0)
    cond = jnp.zeros((N_COND, D), F32).at[0].set(c_ctx).at[1:1 + DEC_BATCH].set(c)
    mod_all = _ada_call(cond, w_ada, b_ada)
    tab_b = _rope_tables(DEC_SEQ, 32, 2)
    tab_c = _rope_tables(DEC_SEQ, 64, 1)

    new_a, new_b, new_c = [], [], []
    for l in range(DEPTH):
        lam_init = 0.8 - 0.6 * math.exp(-0.3 * l)
        mod = mod_all[l].reshape(N_COND, 6, D).transpose(1, 0, 2).reshape(6 * N_COND, 1, D)
        proj = _inproj_call(x, mod, _prep_w_in(w_in[l]))

        conv_w = jnp.pad(a_conv[l], ((0, 8 - CONV_K), (0, 0)))
        alog_row, dtb_row = _ba_row(a_log[l]), _ba_row(a_dt_bias[l])
        an = a_norm[l].reshape(1, HD)
        oa_p, s_new = _delta_call(proj, l, conv_w, alog_row, dtb_row, an, stream="p")
        (oa_s,) = _delta_call(proj, l, conv_w, alog_row, dtb_row, an, stream="s", state=state_a)

        bn = b_norm[l].reshape(1, HD)
        ob_p, kv_b = _attn_b_call(proj, l, b_lambda[l], bn, lam_init, stream="p")
        (ob_s,) = _attn_b_call(proj, l, b_lambda[l], bn, lam_init, stream="s", cache=cache_b_kv, tables=tab_b)

        qn, kn = c_q_norm[l].reshape(1, HD), c_k_norm[l].reshape(1, HD)
        oc_p, kv_c = _attn_c_call(proj, l, qn, kn, stream="p")
        (oc_s,) = _attn_c_call(proj, l, qn, kn, stream="s", cache=cache_c_kv, tables=tab_c)
        new_a.append(s_new)
        new_b.append(kv_b)
        new_c.append(kv_c)

        o_a = jnp.concatenate([oa_p, oa_s], axis=0)
        o_b = jnp.concatenate([ob_p, ob_s], axis=0)
        o_c = jnp.concatenate([oc_p, oc_s], axis=0)
        w_rt = jnp.pad(jnp.concatenate([w_group[l], w_router[l]], axis=1), ((0, 0), (0, HD - N_GROUPS - N_EXPERTS)))
        b_rt = jnp.pad(jnp.concatenate([b_group[l], b_router[l]]), (0, HD - N_GROUPS - N_EXPERTS)).reshape(1, HD)
        x1, h2, route = _outproj_call(o_a, o_b, o_c, x, w_out[l].astype(BF16), mod,
                                      ln_g[l, 0].reshape(1, D), ln_b[l, 0].reshape(1, D), w_rt, b_rt)

        dest, src, tile_expert, n_valid = _dispatch_plan(route)
        xs = jnp.take(h2, src, axis=0)
        ys = _moe_call(tile_expert, n_valid, xs, w_gate[l].astype(BF16), w_up[l].astype(BF16),
                       w_down[l].astype(BF16))
        dest2 = dest.reshape(T_ALL, 2)
        y0 = jnp.take(ys, dest2[:, 0], axis=0)
        y1 = jnp.take(ys, dest2[:, 1], axis=0)
        x = _final_call(x1, y0, y1, route, mod, ln_g[l, 1].reshape(1, D), ln_b[l, 1].reshape(1, D))

    y_p = x[:T_P].reshape(BATCH, SEQ, D)
    y_s = x[T_P:].reshape(DEC_BATCH, DEC_SEQ, D)
    return (y_p, y_s, jnp.stack(new_a, axis=1), jnp.stack(new_b, axis=1), jnp.stack(new_c, axis=1))
```
